```python
import math
import jax, jax.numpy as jnp
from jax import lax
import numpy as np

D_MODEL = 1024
BATCH = 8
SEQ = 2048
DEPTH = 4
DEC_BATCH = 128
DEC_SEQ = 4
PAST_LEN = 8192
PAGE_SIZE = 128

SB_HEADS = 8
SB_KV_HEADS = 2
SB_HEAD_DIM = 64
SB_GROUP = SB_HEADS // SB_KV_HEADS
SB_WIDTH = SB_HEADS * SB_HEAD_DIM
SB_KV_WIDTH = SB_KV_HEADS * SB_HEAD_DIM
HG_HEADS = 4
HG_DK = 128
HG_DV = 128
HG_WIDTH = HG_HEADS * HG_DK
HG_VWIDTH = HG_HEADS * HG_DV
HG_CHUNK = 64
MLA_HEADS = 8
MLA_Q_LORA = 384
MLA_KV_LORA = 256
MLA_D_NOPE = 64
MLA_D_ROPE = 32
MLA_D_V = 64
MLA_WIDTH = MLA_HEADS * MLA_D_V
ROPE_BASE = 10000.0
N_BRANCH = 3
D_FF = 4 * D_MODEL
Q_BLOCK = 128
EPS = 1e-6
NEG_INF = -1e30

IN_SPLITS = (SB_WIDTH, SB_KV_WIDTH, SB_KV_WIDTH,
             HG_WIDTH, HG_WIDTH, HG_VWIDTH, HG_VWIDTH,
             MLA_Q_LORA, MLA_KV_LORA, MLA_D_ROPE,
             N_BRANCH * D_MODEL)
N_IN = sum(IN_SPLITS)
SPLIT_IDX = tuple(int(v) for v in np.cumsum(IN_SPLITS)[:-1])

kernel_name = "hybrid_sb_hgrn2_mla_gated_decoder_step"


def rmsnorm(x, g):
    xf = x.astype(jnp.float32)
    y = xf * lax.rsqrt(jnp.mean(xf * xf, axis=-1, keepdims=True) + EPS)
    return (y * g.astype(jnp.float32)).astype(x.dtype)


def rope(x, pos):
    half = x.shape[-1] // 2
    inv = ROPE_BASE ** (-jnp.arange(half, dtype=jnp.float32) / half)
    ang = pos.astype(jnp.float32)[:, None] * inv[None, :]
    shape = (1, pos.shape[0]) + (1,) * (x.ndim - 3) + (half,)
    c = jnp.cos(ang).reshape(shape)
    s = jnp.sin(ang).reshape(shape)
    xf = x.astype(jnp.float32)
    x1, x2 = xf[..., :half], xf[..., half:]
    return jnp.concatenate([x1 * c - x2 * s, x2 * c + x1 * s], axis=-1).astype(x.dtype)


def stick_breaking_block(q, k, v, q_pos, k_pos):
    z = jnp.einsum("bqhgd,bkhd->bhgqk", q, k).astype(jnp.float32) * (1.0 / math.sqrt(SB_HEAD_DIM))
    mask = k_pos[None, :] < q_pos[:, None]
    log_beta = jax.nn.log_sigmoid(z)
    log_1m_beta = jnp.where(mask, jax.nn.log_sigmoid(-z), 0.0)
    after = lax.cumsum(log_1m_beta, axis=4, reverse=True) - log_1m_beta
    w = jnp.where(mask, jnp.exp(log_beta + after), 0.0)
    return jnp.einsum("bhgqk,bkhd->bqhgd", w.astype(v.dtype), v)


def mla_block(q, k, v, q_pos, k_pos):
    s = jnp.einsum("bqhd,bkhd->bhqk", q, k).astype(jnp.float32) * (1.0 / math.sqrt(MLA_D_NOPE + MLA_D_ROPE))
    mask = k_pos[None, :] <= q_pos[:, None]
    p = jax.nn.softmax(jnp.where(mask, s, NEG_INF), axis=-1)
    return jnp.einsum("bhqk,bkhd->bqhd", p.astype(v.dtype), v)


def attend(block_fn, q, k, v, q_pos, k_pos):
    B, T = q.shape[:2]
    if T <= Q_BLOCK or T % Q_BLOCK:
        return block_fn(q, k, v, q_pos, k_pos)
    nb = T // Q_BLOCK
    qb = jnp.moveaxis(q.reshape((B, nb, Q_BLOCK) + q.shape[2:]), 1, 0)
    pb = q_pos.reshape(nb, Q_BLOCK)
    ob = lax.map(lambda a: block_fn(a[0], k, v, a[1], k_pos), (qb, pb))
    return jnp.moveaxis(ob, 0, 1).reshape((B, T) + ob.shape[3:])


def mla_keys(ckv, kr, w_ukv, kn_g):
    B, Tk, _ = ckv.shape
    kv = (ckv @ w_ukv).reshape(B, Tk, MLA_HEADS, MLA_D_NOPE + MLA_D_V)
    k_nope = rmsnorm(kv[..., :MLA_D_NOPE], kn_g)
    k_rope = jnp.broadcast_to(kr[:, :, None, :], (B, Tk, MLA_HEADS, MLA_D_ROPE)).astype(k_nope.dtype)
    return jnp.concatenate([k_nope, k_rope], axis=-1), kv[..., MLA_D_NOPE:]


def hgrn2_scan(q, log_f, i, s0):
    out_dtype = q.dtype
    B, T, H, _ = q.shape
    C = min(HG_CHUNK, T)
    pad = (-T) % C
    qf = q.astype(jnp.float32)
    lf = log_f.astype(jnp.float32)
    vf = i.astype(jnp.float32)
    if pad:
        cfg = ((0, 0), (0, pad), (0, 0), (0, 0))
        qf, lf, vf = jnp.pad(qf, cfg), jnp.pad(lf, cfg), jnp.pad(vf, cfg)
    kf = -jnp.expm1(lf)
    Tp = T + pad
    n = Tp // C

    def to_chunks(a):
        return a.reshape(B, n, C, H, a.shape[-1]).transpose(1, 0, 3, 2, 4)

    causal = jnp.tril(jnp.ones((C, C), dtype=bool))

    def step(S, xs):
        qc, lfc, kc, vc = xs
        b = jnp.cumsum(lfc, axis=2)
        diff = b[:, :, :, None, :] - b[:, :, None, :, :]
        decay = jnp.where(causal[:, :, None], jnp.exp(jnp.minimum(diff, 0.0)), 0.0)
        a = jnp.einsum("bhtd,bhtsd,bhsd->bhts", qc, decay, kc)
        o = jnp.einsum("bhts,bhsv->bhtv", a, vc) + jnp.einsum("bhtd,bhdv->bhtv", qc * jnp.exp(b), S)
        b_last = b[:, :, -1:, :]
        S_new = jnp.exp(b_last[:, :, 0, :])[..., None] * S + jnp.einsum("bhsd,bhsv->bhdv", kc * jnp.exp(b_last - b), vc)
        return S_new, o

    s_fin, o = lax.scan(step, s0.astype(jnp.float32), (to_chunks(qf), to_chunks(lf), to_chunks(kf), to_chunks(vf)))
    o = o.transpose(1, 0, 3, 2, 4).reshape(B, Tp, H, vf.shape[-1])[:, :T]
    return o.astype(out_dtype), s_fin.astype(s0.dtype)


def trunk_layer(h, pos, past, s0, lb, p):
    B, T, _ = h.shape
    xn = rmsnorm(h, p["norm1_g"])
    proj = xn @ p["w_in"]
    (a_q, a_k, a_v, b_f, b_q, b_i, b_g, c_q, c_kv, c_kr, gate_logits) = jnp.split(proj, SPLIT_IDX, axis=-1)

    q_a = a_q.reshape(B, T, SB_KV_HEADS, SB_GROUP, SB_HEAD_DIM)
    k_a = a_k.reshape(B, T, SB_KV_HEADS, SB_HEAD_DIM)
    v_a = a_v.reshape(B, T, SB_KV_HEADS, SB_HEAD_DIM)

    cq = rmsnorm(c_q, p["mla_q_norm_g"])
    q_c = (cq @ p["mla_w_uq"]).reshape(B, T, MLA_HEADS, MLA_D_NOPE + MLA_D_ROPE)
    q_c = jnp.concatenate([rmsnorm(q_c[..., :MLA_D_NOPE], p["mla_qn_g"]),
                           rope(rmsnorm(q_c[..., MLA_D_NOPE:], p["mla_qr_g"]), pos)], axis=-1)
    ckv = rmsnorm(c_kv, p["mla_kv_norm_g"])
    kr = rope(rmsnorm(c_kr, p["mla_kr_g"]), pos)

    if past is None:
        k_a_all, v_a_all, ckv_all, kr_all, k_pos = k_a, v_a, ckv, kr, pos
    else:
        k_past, v_past, c_past, kr_past = past
        k_a_all = jnp.concatenate([k_past.astype(k_a.dtype), k_a], axis=1)
        v_a_all = jnp.concatenate([v_past.astype(v_a.dtype), v_a], axis=1)
        ckv_all = jnp.concatenate([c_past.astype(ckv.dtype), ckv], axis=1)
        kr_all = jnp.concatenate([kr_past.astype(kr.dtype), kr], axis=1)
        k_pos = jnp.arange(k_a_all.shape[1], dtype=jnp.int32)

    o_a = attend(stick_breaking_block, q_a, k_a_all, v_a_all, pos, k_pos).reshape(B, T, SB_WIDTH)
    k_c, v_c = mla_keys(ckv_all, kr_all, p["mla_w_ukv"], p["mla_kn_g"])
    o_c = attend(mla_block, q_c, k_c, v_c, pos, k_pos).reshape(B, T, MLA_WIDTH)

    fl = b_f.astype(jnp.float32).reshape(B, T, HG_HEADS, HG_DK)
    lbh = lb.reshape(HG_HEADS, HG_DK).astype(jnp.float32)
    log_f = jnp.log(lbh + (1.0 - lbh) * jax.nn.sigmoid(fl))
    q_b = jax.nn.silu(b_q.reshape(B, T, HG_HEADS, HG_DK))
    i_b = b_i.reshape(B, T, HG_HEADS, HG_DV)
    o_b, s_new = hgrn2_scan(q_b, log_f, i_b, s0)
    o_b = (rmsnorm(o_b, p["hg_out_norm_g"]) * jax.nn.silu(b_g.reshape(B, T, HG_HEADS, HG_DV))).reshape(B, T, HG_VWIDTH)

    g = jax.nn.sigmoid(gate_logits).reshape(B, T, N_BRANCH, D_MODEL)
    merged = (g[:, :, 0] * (o_a @ p["w_br_a"]) + g[:, :, 1] * (o_b @ p["w_br_b"])
              + g[:, :, 2] * (o_c @ p["w_br_c"]))
    h = h + merged @ p["w_out"]

    u = rmsnorm(h, p["norm2_g"]) @ p["w_up"]
    h = h + jnp.square(jax.nn.relu(u)) @ p["w_down"]
    return h, (k_a, v_a, ckv, kr, s_new)


def gather_pages(pool, page_table):
    g = pool[page_table]
    return g.reshape((page_table.shape[0], page_table.shape[1] * pool.shape[1]) + pool.shape[2:])


def setup_inputs(seed: int = 0) -> dict:
    key = jax.random.key(seed)
    ks = iter(jax.random.split(key, 40))
    n_pages = PAST_LEN // PAGE_SIZE
    n_used = DEC_BATCH * n_pages
    n_pool = (n_used * 5) // 4

    def nrm(shape, scale):
        return jax.random.normal(next(ks), shape, jnp.float32) * scale

    def gain(shape):
        return 1.0 + 0.05 * jax.random.normal(next(ks), shape, jnp.float32)

    x_prompt = nrm((BATCH, SEQ, D_MODEL), 1.0)
    x_sample = nrm((DEC_BATCH, DEC_SEQ, D_MODEL), 1.0)
    cache_sb_k = nrm((DEPTH, n_pool, PAGE_SIZE, SB_KV_HEADS, SB_HEAD_DIM), 1.0)
    cache_sb_v = nrm((DEPTH, n_pool, PAGE_SIZE, SB_KV_HEADS, SB_HEAD_DIM), 1.0)
    cache_mla_c = nrm((DEPTH, n_pool, PAGE_SIZE, MLA_KV_LORA), 1.0)
    cache_mla_kr = nrm((DEPTH, n_pool, PAGE_SIZE, MLA_D_ROPE), 1.0)
    state_hgrn = nrm((DEPTH, DEC_BATCH, HG_HEADS, HG_DK, HG_DV), 0.5)
    page_table = jax.random.permutation(next(ks), n_pool)[:n_used].reshape(DEC_BATCH, n_pages).astype(jnp.int32)

    return {
        "x_prompt": x_prompt,
        "x_sample": x_sample,
        "cache_sb_k": cache_sb_k,
        "cache_sb_v": cache_sb_v,
        "cache_mla_c": cache_mla_c,
        "cache_mla_kr": cache_mla_kr,
        "state_hgrn": state_hgrn,
        "page_table": page_table,
        "norm1_g": gain((DEPTH, D_MODEL)),
        "w_in": nrm((DEPTH, D_MODEL, N_IN), D_MODEL ** -0.5),
        "hg_lower_bound": nrm((DEPTH, HG_WIDTH), 0.5),
        "hg_out_norm_g": gain((DEPTH, HG_HEADS, HG_DV)),
        "mla_q_norm_g": gain((DEPTH, MLA_Q_LORA)),
        "mla_kv_norm_g": gain((DEPTH, MLA_KV_LORA)),
        "mla_w_uq": nrm((DEPTH, MLA_Q_LORA, MLA_HEADS * (MLA_D_NOPE + MLA_D_ROPE)), MLA_Q_LORA ** -0.5),
        "mla_w_ukv": nrm((DEPTH, MLA_KV_LORA, MLA_HEADS * (MLA_D_NOPE + MLA_D_V)), MLA_KV_LORA ** -0.5),
        "mla_qn_g": gain((DEPTH, MLA_D_NOPE)),
        "mla_kn_g": gain((DEPTH, MLA_D_NOPE)),
        "mla_qr_g": gain((DEPTH, MLA_D_ROPE)),
        "mla_kr_g": gain((DEPTH, MLA_D_ROPE)),
        "w_br_a": nrm((DEPTH, SB_WIDTH, D_MODEL), SB_WIDTH ** -0.5),
        "w_br_b": nrm((DEPTH, HG_VWIDTH, D_MODEL), HG_VWIDTH ** -0.5),
        "w_br_c": nrm((DEPTH, MLA_WIDTH, D_MODEL), MLA_WIDTH ** -0.5),
        "w_out": nrm((DEPTH, D_MODEL, D_MODEL), D_MODEL ** -0.5),
        "norm2_g": gain((DEPTH, D_MODEL)),
        "w_up": nrm((DEPTH, D_MODEL, D_FF), D_MODEL ** -0.5),
        "w_down": nrm((DEPTH, D_FF, D_MODEL), D_FF ** -0.5),
    }


def reference(x_prompt, x_sample, cache_sb_k, cache_sb_v, cache_mla_c, cache_mla_kr, state_hgrn, page_table,
              norm1_g, w_in, hg_lower_bound, hg_out_norm_g, mla_q_norm_g, mla_kv_norm_g, mla_w_uq, mla_w_ukv,
              mla_qn_g, mla_kn_g, mla_qr_g, mla_kr_g, w_br_a, w_br_b, w_br_c, w_out, norm2_g, w_up, w_down):
    lb_soft = jax.nn.softmax(hg_lower_bound.astype(jnp.float32), axis=0)
    lower_bounds = jnp.cumsum(lb_soft, axis=0) - lb_soft[0]

    past_len = page_table.shape[1] * cache_sb_k.shape[2]
    pos_p = jnp.arange(x_prompt.shape[1], dtype=jnp.int32)
    pos_s = past_len + jnp.arange(x_sample.shape[1], dtype=jnp.int32)
    s0_p = jnp.zeros((x_prompt.shape[0], HG_HEADS, HG_DK, HG_DV), x_prompt.dtype)

    hp, hs = x_prompt, x_sample
    new_p, new_s = [], []
    for l in range(DEPTH):
        p = {
            "norm1_g": norm1_g[l], "w_in": w_in[l], "hg_out_norm_g": hg_out_norm_g[l],
            "mla_q_norm_g": mla_q_norm_g[l], "mla_kv_norm_g": mla_kv_norm_g[l],
            "mla_w_uq": mla_w_uq[l], "mla_w_ukv": mla_w_ukv[l],
            "mla_qn_g": mla_qn_g[l], "mla_kn_g": mla_kn_g[l], "mla_qr_g": mla_qr_g[l], "mla_kr_g": mla_kr_g[l],
            "w_br_a": w_br_a[l], "w_br_b": w_br_b[l], "w_br_c": w_br_c[l], "w_out": w_out[l],
            "norm2_g": norm2_g[l], "w_up": w_up[l], "w_down": w_down[l],
        }
        hp, st_p = trunk_layer(hp, pos_p, None, s0_p, lower_bounds[l], p)
        past = (gather_pages(cache_sb_k[l], page_table), gather_pages(cache_sb_v[l], page_table),
                gather_pages(cache_mla_c[l], page_table), gather_pages(cache_mla_kr[l], page_table))
        hs, st_s = trunk_layer(hs, pos_s, past, state_hgrn[l], lower_bounds[l], p)
        new_p.append(st_p)
        new_s.append(st_s)

    sb_k_prompt = jnp.stack([s[0] for s in new_p])
    sb_v_prompt = jnp.stack([s[1] for s in new_p])
    mla_c_prompt = jnp.stack([s[2] for s in new_p])
    mla_kr_prompt = jnp.stack([s[3] for s in new_p])
    hgrn_state_prompt = jnp.stack([s[4] for s in new_p])
    sb_k_sample = jnp.stack([s[0] for s in new_s])
    sb_v_sample = jnp.stack([s[1] for s in new_s])
    mla_c_sample = jnp.stack([s[2] for s in new_s])
    mla_kr_sample = jnp.stack([s[3] for s in new_s])
    hgrn_state_sample = jnp.stack([s[4] for s in new_s])
    return (hp, hs, sb_k_prompt, sb_v_prompt, mla_c_prompt, mla_kr_prompt, hgrn_state_prompt,
            sb_k_sample, sb_v_sample, mla_c_sample, mla_kr_sample, hgrn_state_sample)
```

```python
import functools
import math

import jax
import jax.numpy as jnp
from jax import lax
from jax.experimental import pallas as pl
from jax.experimental.pallas import tpu as pltpu

F32 = jnp.float32
BF16 = jnp.bfloat16

SB_HEADS, SB_KV_HEADS, SB_HEAD_DIM = 8, 2, 64
SB_GROUP = SB_HEADS // SB_KV_HEADS
HG_HEADS, HG_DK, HG_DV = 4, 128, 128
MLA_HEADS, MLA_Q_LORA, MLA_KV_LORA = 8, 384, 256
MLA_D_NOPE, MLA_D_ROPE, MLA_D_V = 64, 32, 64
ROPE_BASE = 10000.0
N_BRANCH = 3
EPS = 1e-6
NEG_INF = -1e30
LANES = 128

COL_AQ = 0
COL_BF = 4
COL_BQ = 8
COL_BI = 12
COL_BG = 16
COL_AKV = 20
COL_CKV = 22
COL_GATE = 24
COL_CQ = 48
COL_KR = 51
N_COLS = 52 * LANES

HG_CHUNK = 128
HG_SUB = 16
SB_TQ = 128
MLA_TQ = 256
DEC_PAGES = 8
VMEM_LIMIT = 48 * 2**20


def _cparams(sem):
    return pltpu.CompilerParams(dimension_semantics=sem, vmem_limit_bytes=VMEM_LIMIT)


def _pick_tile(m, candidates):
    for c in candidates:
        if m % c == 0:
            return c
    raise ValueError(f"no tile for {m} in {candidates}")


def _sigmoid(x):
    return 1.0 / (1.0 + jnp.exp(-x))


def _split3(x):
    hi = x.astype(BF16)
    r1 = x - hi.astype(F32)
    mid = r1.astype(BF16)
    lo = (r1 - mid.astype(F32)).astype(BF16)
    return hi, mid, lo


def _nt_dot(a, b):
    return lax.dot_general(a, b, (((1,), (1,)), ((), ())), preferred_element_type=F32)


def _tn_dot(a, b):
    return lax.dot_general(a, b, (((0,), (0,)), ((), ())), preferred_element_type=F32)


def _dot(a, b):
    return jnp.dot(a, b, preferred_element_type=F32)


def _rms_matmul_kernel(x_ref, g_ref, w_ref, o_ref, xn_ref):
    @pl.when(pl.program_id(1) == 0)
    def _():
        x = x_ref[...]
        ms = jnp.mean(x * x, axis=-1, keepdims=True)
        xn_ref[...] = (x * lax.rsqrt(ms + EPS) * g_ref[...]).astype(BF16)

    o_ref[...] = _dot(xn_ref[...], w_ref[...])


def rms_matmul(x, g, w):
    m, d = x.shape
    n = w.shape[1]
    tm = _pick_tile(m, (768, 512, 256, 128, 96, 32, 8))
    tn = _pick_tile(n, (512, 256, 128))
    return pl.pallas_call(
        _rms_matmul_kernel,
        grid=(m // tm, n // tn),
        in_specs=[pl.BlockSpec((tm, d), lambda i, j: (i, 0)),
                  pl.BlockSpec((1, d), lambda i, j: (0, 0)),
                  pl.BlockSpec((d, tn), lambda i, j: (0, j))],
        out_specs=pl.BlockSpec((tm, tn), lambda i, j: (i, j)),
        out_shape=jax.ShapeDtypeStruct((m, n), F32),
        scratch_shapes=[pltpu.VMEM((tm, d), BF16)],
        compiler_params=_cparams(("parallel", "arbitrary")),
    )(x, g, w)


def _merge_kernel(h_ref, oa_ref, ob_ref, oc_ref, gate_ref, wa_ref, wb_ref, wc_ref, wo_ref, o_ref):
    d = h_ref.shape[1]
    m = _sigmoid(gate_ref[:, 0:d]) * _dot(oa_ref[...], wa_ref[...])
    m = m + _sigmoid(gate_ref[:, d:2 * d]) * _dot(ob_ref[...], wb_ref[...])
    m = m + _sigmoid(gate_ref[:, 2 * d:3 * d]) * _dot(oc_ref[...], wc_ref[...])
    o_ref[...] = h_ref[...] + _dot(m.astype(BF16), wo_ref[...])


def merge_out(h, oa, ob, oc, proj, wa, wb, wc, wo):
    m, d = h.shape
    tm = _pick_tile(m, (256, 128, 96, 32, 16))
    gate_blk = COL_GATE * LANES // (N_BRANCH * d)
    row = lambda i: (i, 0)
    const = lambda i: (0, 0)
    return pl.pallas_call(
        _merge_kernel,
        grid=(m // tm,),
        in_specs=[pl.BlockSpec((tm, d), row),
                  pl.BlockSpec((tm, oa.shape[1]), row),
                  pl.BlockSpec((tm, ob.shape[1]), row),
                  pl.BlockSpec((tm, oc.shape[1]), row),
                  pl.BlockSpec((tm, N_BRANCH * d), lambda i: (i, gate_blk)),
                  pl.BlockSpec(wa.shape, const), pl.BlockSpec(wb.shape, const),
                  pl.BlockSpec(wc.shape, const), pl.BlockSpec(wo.shape, const)],
        out_specs=pl.BlockSpec((tm, d), row),
        out_shape=jax.ShapeDtypeStruct((m, d), F32),
        compiler_params=_cparams(("parallel",)),
    )(h, oa, ob, oc, proj, wa, wb, wc, wo)


def _mlp_kernel(h_ref, g_ref, wu_ref, wd_ref, o_ref, xn_ref, acc_ref):
    f = pl.program_id(1)

    @pl.when(f == 0)
    def _():
        x = h_ref[...]
        ms = jnp.mean(x * x, axis=-1, keepdims=True)
        xn_ref[...] = (x * lax.rsqrt(ms + EPS) * g_ref[...]).astype(BF16)
        acc_ref[...] = x

    u = jnp.maximum(_dot(xn_ref[...], wu_ref[...]), 0.0)
    acc_ref[...] += _dot((u * u).astype(BF16), wd_ref[...])

    @pl.when(f == pl.num_programs(1) - 1)
    def _():
        o_ref[...] = acc_ref[...]


def mlp(h, g, wu, wd):
    m, d = h.shape
    ff = wu.shape[1]
    tm = _pick_tile(m, (768, 512, 256, 128, 96, 32, 8))
    tf = _pick_tile(ff, (512, 256, 128))
    return pl.pallas_call(
        _mlp_kernel,
        grid=(m // tm, ff // tf),
        in_specs=[pl.BlockSpec((tm, d), lambda i, f: (i, 0)),
                  pl.BlockSpec((1, d), lambda i, f: (0, 0)),
                  pl.BlockSpec((d, tf), lambda i, f: (0, f)),
                  pl.BlockSpec((tf, d), lambda i, f: (f, 0))],
        out_specs=pl.BlockSpec((tm, d), lambda i, f: (i, 0)),
        out_shape=jax.ShapeDtypeStruct((m, d), F32),
        scratch_shapes=[pltpu.VMEM((tm, d), BF16), pltpu.VMEM((tm, d), F32)],
        compiler_params=_cparams(("parallel", "arbitrary")),
    )(h, g, wu, wd)


def _rope128(y, cos, sin, lane):
    half = MLA_D_ROPE // 2
    swapped = jnp.where(lane < MLA_D_NOPE + half,
                        pltpu.roll(y, LANES - half, 1), pltpu.roll(y, half, 1))
    return y * cos + swapped * sin


def _mla_prep_kernel(cq_ref, ckv_ref, kr_ref, gq_ref, gkv_ref, qgain_ref, krgain_ref,
                     cos_ref, sin_ref, wuq_ref, wkn_ref, wv_ref,
                     q_out, ckv_out, kr_out, k_out, v_out):
    tm = cq_ref.shape[0]
    lane = lax.broadcasted_iota(jnp.int32, (tm, LANES), 1)
    is_nope = lane < MLA_D_NOPE
    cos = cos_ref[...]
    sin = sin_ref[...]

    x = kr_ref[...]
    ms = jnp.sum(x * x, axis=-1, keepdims=True) * (1.0 / MLA_D_ROPE)
    kr128 = _rope128(x * lax.rsqrt(ms + EPS) * krgain_ref[...], cos, sin, lane)
    kr_out[...] = kr128[:, MLA_D_NOPE:MLA_D_NOPE + MLA_D_ROPE]

    x = cq_ref[...]
    ms = jnp.mean(x * x, axis=-1, keepdims=True)
    cq = (x * lax.rsqrt(ms + EPS) * gq_ref[...]).astype(BF16)
    qraw = _dot(cq, wuq_ref[...])
    for h in range(MLA_HEADS):
        x = qraw[:, h * LANES:(h + 1) * LANES]
        sq = x * x
        ms_n = jnp.sum(jnp.where(is_nope, sq, 0.0), axis=-1, keepdims=True) * (1.0 / MLA_D_NOPE)
        ms_r = jnp.sum(jnp.where(is_nope, 0.0, sq), axis=-1, keepdims=True) * (1.0 / MLA_D_ROPE)
        r = jnp.where(is_nope, lax.rsqrt(ms_n + EPS), lax.rsqrt(ms_r + EPS))
        y = _rope128(x * r * qgain_ref[...], cos, sin, lane)
        q_out[:, h * LANES:(h + 1) * LANES] = y.astype(BF16)

    x = ckv_ref[...]
    ms = jnp.mean(x * x, axis=-1, keepdims=True)
    ckv = x * lax.rsqrt(ms + EPS) * gkv_ref[...]
    ckv_out[...] = ckv
    cb = ckv.astype(BF16)
    kn = _dot(cb, wkn_ref[...])
    for h in range(MLA_HEADS):
        x = kn[:, h * LANES:(h + 1) * LANES]
        ms = jnp.sum(x * x, axis=-1, keepdims=True) * (1.0 / MLA_D_NOPE)
        k_out[:, h * LANES:(h + 1) * LANES] = (x * lax.rsqrt(ms + EPS) + kr128).astype(BF16)
    v_out[...] = _dot(cb, wv_ref[...]).astype(BF16)


def mla_prep(proj, gq, gkv, qgain, krgain, cos, sin, wuq, wkn, wv):
    m = proj.shape[0]
    tm = _pick_tile(m, (256, 128, 96, 32, 16))
    hw = MLA_HEADS * LANES
    row = lambda i: (i, 0)
    const = lambda i: (0, 0)
    return pl.pallas_call(
        _mla_prep_kernel,
        grid=(m // tm,),
        in_specs=[pl.BlockSpec((tm, MLA_Q_LORA), lambda i: (i, COL_CQ * LANES // MLA_Q_LORA)),
                  pl.BlockSpec((tm, MLA_KV_LORA), lambda i: (i, COL_CKV * LANES // MLA_KV_LORA)),
                  pl.BlockSpec((tm, LANES), lambda i: (i, COL_KR)),
                  pl.BlockSpec((1, MLA_Q_LORA), const), pl.BlockSpec((1, MLA_KV_LORA), const),
                  pl.BlockSpec((1, LANES), const), pl.BlockSpec((1, LANES), const),
                  pl.BlockSpec((tm, LANES), row), pl.BlockSpec((tm, LANES), row),
                  pl.BlockSpec(wuq.shape, const), pl.BlockSpec(wkn.shape, const),
                  pl.BlockSpec(wv.shape, const)],
        out_specs=[pl.BlockSpec((tm, hw), row), pl.BlockSpec((tm, MLA_KV_LORA), row),
                   pl.BlockSpec((tm, MLA_D_ROPE), row), pl.BlockSpec((tm, hw), row),
                   pl.BlockSpec((tm, MLA_HEADS * MLA_D_V), row)],
        out_shape=[jax.ShapeDtypeStruct((m, hw), BF16),
                   jax.ShapeDtypeStruct((m, MLA_KV_LORA), F32),
                   jax.ShapeDtypeStruct((m, MLA_D_ROPE), F32),
                   jax.ShapeDtypeStruct((m, hw), BF16),
                   jax.ShapeDtypeStruct((m, MLA_HEADS * MLA_D_V), BF16)],
        compiler_params=_cparams(("parallel",)),
    )(proj, proj, proj, gq, gkv, qgain, krgain, cos, sin, wuq, wkn, wv)


def _log_sigmoid_pair(z):
    lb = jnp.minimum(z, 0.0) - jnp.log(1.0 + jnp.exp(-jnp.abs(z)))
    return lb, lb - z


def _sb_prompt_kernel(q_ref, kv_ref, u_ref, o_ref, *, tq):
    i = pl.program_id(1)
    dh = SB_HEAD_DIM
    rows = SB_GROUP * tq
    t_idx = lax.broadcasted_iota(jnp.int32, (rows, tq), 0) % tq
    s_idx = lax.broadcasted_iota(jnp.int32, (rows, tq), 1)
    strict = s_idx < t_idx
    u = u_ref[...]
    kv_w = SB_KV_HEADS * dh

    for kvh in range(SB_KV_HEADS):
        qs = jnp.concatenate(
            [q_ref[:, (kvh * SB_GROUP + g) * dh:(kvh * SB_GROUP + g + 1) * dh] for g in range(SB_GROUP)],
            axis=0)
        qs = (qs * (1.0 / math.sqrt(dh))).astype(BF16)

        def block(j, acc, cb, masked):
            r0 = pl.multiple_of(j * tq, tq)
            k = kv_ref[pl.ds(r0, tq), kvh * dh:(kvh + 1) * dh].astype(BF16)
            v = kv_ref[pl.ds(r0, tq), kv_w + kvh * dh:kv_w + (kvh + 1) * dh].astype(BF16)
            lb, l1m = _log_sigmoid_pair(_nt_dot(qs, k))
            if masked:
                l1m = jnp.where(strict, l1m, 0.0)
            hi = l1m.astype(BF16)
            lo = (l1m - hi.astype(F32)).astype(BF16)
            r = _dot(jnp.concatenate([hi, lo], axis=0), u)
            r = r[:rows] + r[rows:]
            w = jnp.exp(lb + r[:, :tq] + cb)
            if masked:
                w = jnp.where(strict, w, 0.0)
            return acc + _dot(w.astype(BF16), v), cb + r[:, tq:]

        acc0 = jnp.zeros((rows, dh), F32)
        cb0 = jnp.zeros((rows, tq), F32)
        acc, cb = block(i, acc0, cb0, True)
        acc, cb = lax.fori_loop(0, i, lambda jj, c: block(i - 1 - jj, c[0], c[1], False), (acc, cb))
        for g in range(SB_GROUP):
            h = kvh * SB_GROUP + g
            o_ref[:, h * dh:(h + 1) * dh] = acc[g * tq:(g + 1) * tq].astype(o_ref.dtype)


def sb_attn_prompt(proj, u2, batch, seq):
    tq = SB_TQ
    nq = seq // tq
    qw = SB_HEADS * SB_HEAD_DIM
    kvw = 2 * SB_KV_HEADS * SB_HEAD_DIM
    return pl.pallas_call(
        functools.partial(_sb_prompt_kernel, tq=tq),
        grid=(batch, nq),
        in_specs=[pl.BlockSpec((tq, qw), lambda b, i: (b * nq + i, COL_AQ * LANES // qw)),
                  pl.BlockSpec((seq, kvw), lambda b, i: (b, COL_AKV * LANES // kvw)),
                  pl.BlockSpec(u2.shape, lambda b, i: (0, 0))],
        out_specs=pl.BlockSpec((tq, qw), lambda b, i: (b * nq + i, 0)),
        out_shape=jax.ShapeDtypeStruct((batch * seq, qw), BF16),
        compiler_params=_cparams(("parallel", "arbitrary")),
    )(proj, proj, u2)


def _sb_decode_kernel(pt_ref, q_ref, knew_ref, vnew_ref, *rest, n_pages, ntok):
    del pt_ref
    kt_refs = rest[:n_pages]
    vt_refs = rest[n_pages:2 * n_pages]
    u_ref, o_ref, acc_ref, cb_ref = rest[2 * n_pages:]
    g = pl.program_id(1)
    q = q_ref[0]
    rows = q.shape[0]
    u = u_ref[...]

    def blocks(z, mask):
        n = z.shape[1] // LANES
        lb, l1m = _log_sigmoid_pair(z)
        if mask is not None:
            l1m = jnp.where(mask, l1m, 0.0)
        hi = l1m.astype(BF16)
        lo = (l1m - hi.astype(F32)).astype(BF16)
        x = jnp.concatenate([hi[:, p * LANES:(p + 1) * LANES] for p in range(n)]
                            + [lo[:, p * LANES:(p + 1) * LANES] for p in range(n)], axis=0)
        r = _dot(x, u)
        cb = cb_ref[...]
        ws = []
        for p in range(n):
            rp = r[p * rows:(p + 1) * rows] + r[(n + p) * rows:(n + p + 1) * rows]
            w = jnp.exp(lb[:, p * LANES:(p + 1) * LANES] + rp[:, :LANES] + cb)
            if mask is not None:
                w = jnp.where(mask[:, p * LANES:(p + 1) * LANES], w, 0.0)
            ws.append(w.astype(BF16))
            cb = cb + rp[:, LANES:]
        cb_ref[...] = cb
        return ws

    @pl.when(g == 0)
    def _():
        acc_ref[...] = jnp.zeros_like(acc_ref)
        cb_ref[...] = jnp.zeros_like(cb_ref)
        zpad = jnp.zeros((LANES - knew_ref.shape[1], LANES), F32)
        kn = jnp.concatenate([knew_ref[0], zpad], axis=0).astype(BF16)
        vn = jnp.concatenate([vnew_ref[0], zpad], axis=0).astype(BF16)
        key = lax.broadcasted_iota(jnp.int32, (rows, LANES), 1)
        tok = lax.broadcasted_iota(jnp.int32, (rows, LANES), 0) % ntok
        (w,) = blocks(_nt_dot(q, kn), key < tok)
        acc_ref[...] += _dot(w, vn)

    kt = jnp.concatenate([kt_refs[p][...].astype(BF16) for p in range(n_pages)], axis=1)
    ws = blocks(_dot(q, kt), None)
    upd = _nt_dot(ws[0], vt_refs[0][...].astype(BF16))
    for p in range(1, n_pages):
        upd = upd + _nt_dot(ws[p], vt_refs[p][...].astype(BF16))
    acc_ref[...] += upd

    @pl.when(g == pl.num_programs(1) - 1)
    def _():
        o_ref[0] = acc_ref[...]


def sb_attn_decode(page_table, q, knew, vnew, cache_kt, cache_vt, layer, u2, ntok):
    nseq, npg = page_table.shape
    n_pages = DEC_PAGES
    steps = npg // n_pages
    rows = q.shape[1]

    def page_spec(p):
        return pl.BlockSpec((None, None, LANES, cache_kt.shape[3]),
                            lambda b, g, pt: (layer, pt[b, npg - 1 - (g * n_pages + p)], 0, 0))

    seq_blk = lambda b, g, pt: (b, 0, 0)
    grid_spec = pltpu.PrefetchScalarGridSpec(
        num_scalar_prefetch=1,
        grid=(nseq, steps),
        in_specs=([pl.BlockSpec((1,) + q.shape[1:], seq_blk),
                   pl.BlockSpec((1,) + knew.shape[1:], seq_blk),
                   pl.BlockSpec((1,) + vnew.shape[1:], seq_blk)]
                  + [page_spec(p) for p in range(n_pages)]
                  + [page_spec(p) for p in range(n_pages)]
                  + [pl.BlockSpec(u2.shape, lambda b, g, pt: (0, 0))]),
        out_specs=pl.BlockSpec((1, rows, LANES), seq_blk),
        scratch_shapes=[pltpu.VMEM((rows, LANES), F32), pltpu.VMEM((rows, LANES), F32)],
    )
    return pl.pallas_call(
        functools.partial(_sb_decode_kernel, n_pages=n_pages, ntok=ntok),
        grid_spec=grid_spec,
        out_shape=jax.ShapeDtypeStruct((nseq, rows, LANES), F32),
        compiler_params=_cparams(("parallel", "arbitrary")),
    )(page_table, q, knew, vnew, *([cache_kt] * n_pages), *([cache_vt] * n_pages), u2)


def _mla_prompt_kernel(q_ref, k_ref, v_ref, o_ref, *, tq):
    i = pl.program_id(2)
    t_idx = lax.broadcasted_iota(jnp.int32, (tq, tq), 0)
    s_idx = lax.broadcasted_iota(jnp.int32, (tq, tq), 1)
    causal = s_idx <= t_idx
    outs = []
    for hh in range(2):
        q = q_ref[:, hh * LANES:(hh + 1) * LANES]

        def block(j, carry, masked):
            m, l, acc = carry
            r0 = pl.multiple_of(j * tq, tq)
            k = k_ref[pl.ds(r0, tq), hh * LANES:(hh + 1) * LANES]
            v = v_ref[pl.ds(r0, tq), hh * MLA_D_V:(hh + 1) * MLA_D_V]
            s = _nt_dot(q, k)
            if masked:
                s = jnp.where(causal, s, NEG_INF)
            m_new = jnp.maximum(m, jnp.max(s, axis=-1, keepdims=True))
            p = jnp.exp(s - m_new)
            alpha = jnp.exp(m - m_new)
            l = alpha * l + jnp.sum(p, axis=-1, keepdims=True)
            acc = alpha * acc + _dot(p.astype(BF16), v)
            return m_new, l, acc

        init = (jnp.full((tq, 1), NEG_INF, F32), jnp.zeros((tq, 1), F32), jnp.zeros((tq, MLA_D_V), F32))
        carry = lax.fori_loop(0, i, lambda j, c: block(j, c, False), init)
        m, l, acc = block(i, carry, True)
        outs.append(acc / l)
    o_ref[...] = jnp.concatenate(outs, axis=-1).astype(o_ref.dtype)


def mla_attn_prompt(q128, k128, v, batch, seq):
    tq = MLA_TQ
    nq = seq // tq
    hp = MLA_HEADS // 2
    return pl.pallas_call(
        functools.partial(_mla_prompt_kernel, tq=tq),
        grid=(batch, hp, nq),
        in_specs=[pl.BlockSpec((tq, 2 * LANES), lambda b, h, i: (b * nq + i, h)),
                  pl.BlockSpec((seq, 2 * LANES), lambda b, h, i: (b, h)),
                  pl.BlockSpec((seq, 2 * MLA_D_V), lambda b, h, i: (b, h))],
        out_specs=pl.BlockSpec((tq, 2 * MLA_D_V), lambda b, h, i: (b * nq + i, h)),
        out_shape=jax.ShapeDtypeStruct((batch * seq, MLA_HEADS * MLA_D_V), BF16),
        compiler_params=_cparams(("parallel", "parallel", "arbitrary")),
    )(q128, k128, v)


def _mla_decode_kernel(pt_ref, qn_ref, qr_ref, cnew_ref, krnew_ref, *rest, n_pages, ntok):
    del pt_ref
    c_refs = rest[:n_pages]
    kr_refs = rest[n_pages:2 * n_pages]
    wkn_ref, bd_ref, ctx_ref, l_ref, m_scr, l_scr, ctx_scr = rest[2 * n_pages:]
    g = pl.program_id(1)
    qn = qn_ref[0]
    qr = qr_ref[0]
    wkn = wkn_ref[...]
    bd = bd_ref[...]

    def scores(c, kr, kr_transposed):
        cb = c.astype(BF16)
        kn = _dot(cb, wkn)
        ss = _dot((kn * kn).astype(BF16), bd)
        s = _dot(kn.astype(BF16), qn) * lax.rsqrt(ss * (1.0 / MLA_D_NOPE) + EPS)
        rope = _tn_dot(kr.astype(BF16), qr) if kr_transposed else _dot(kr.astype(BF16), qr)
        return s + rope, cb

    def update(s, cb):
        m_old = m_scr[0:1]
        m_new = jnp.maximum(m_old, jnp.max(s, axis=0, keepdims=True))
        p = jnp.exp(s - m_new)
        alpha = jnp.exp(m_old - m_new)
        l_scr[...] = jnp.broadcast_to(alpha * l_scr[0:1] + jnp.sum(p, axis=0, keepdims=True), l_scr.shape)
        m_scr[...] = jnp.broadcast_to(m_new, m_scr.shape)
        a_col = jnp.transpose(jnp.broadcast_to(alpha, (LANES, LANES)))
        a_col = jnp.concatenate([a_col] * (MLA_KV_LORA // LANES), axis=1)
        ctx_scr[...] = ctx_scr[...] * a_col + _tn_dot(p.astype(BF16), cb)

    @pl.when(g == 0)
    def _():
        m_scr[...] = jnp.full_like(m_scr, NEG_INF)
        l_scr[...] = jnp.zeros_like(l_scr)
        ctx_scr[...] = jnp.zeros_like(ctx_scr)
        s, cb = scores(cnew_ref[0], krnew_ref[0], False)
        key = lax.broadcasted_iota(jnp.int32, s.shape, 0)
        col = lax.broadcasted_iota(jnp.int32, s.shape, 1)
        update(jnp.where(key <= col % ntok, s, NEG_INF), cb)

    parts = [scores(c_refs[p][...], kr_refs[p][...], True) for p in range(n_pages)]
    update(jnp.concatenate([s for s, _ in parts], axis=0),
           jnp.concatenate([cb for _, cb in parts], axis=0))

    @pl.when(g == pl.num_programs(1) - 1)
    def _():
        ctx_ref[0] = ctx_scr[0:ctx_ref.shape[1]]
        l_ref[0] = l_scr[...]


def mla_attn_decode(page_table, qn, qr, cnew, krnew, cache_c, cache_krt, layer, wkn, bd, ntok):
    nseq, npg = page_table.shape
    n_pages = DEC_PAGES
    steps = npg // n_pages
    ncol = MLA_HEADS * ntok

    def page_spec(p, arr):
        return pl.BlockSpec((None, None) + arr.shape[2:],
                            lambda b, g, pt: (layer, pt[b, npg - 1 - (g * n_pages + p)], 0, 0))

    seq_blk = lambda b, g, pt: (b, 0, 0)
    const = lambda b, g, pt: (0, 0)
    grid_spec = pltpu.PrefetchScalarGridSpec(
        num_scalar_prefetch=1,
        grid=(nseq, steps),
        in_specs=([pl.BlockSpec((1,) + qn.shape[1:], seq_blk), pl.BlockSpec((1,) + qr.shape[1:], seq_blk),
                   pl.BlockSpec((1,) + cnew.shape[1:], seq_blk), pl.BlockSpec((1,) + krnew.shape[1:], seq_blk)]
                  + [page_spec(p, cache_c) for p in range(n_pages)]
                  + [page_spec(p, cache_krt) for p in range(n_pages)]
                  + [pl.BlockSpec(wkn.shape, const), pl.BlockSpec(bd.shape, const)]),
        out_specs=[pl.BlockSpec((1, ncol, MLA_KV_LORA), seq_blk), pl.BlockSpec((1, 8, LANES), seq_blk)],
        scratch_shapes=[pltpu.VMEM((8, LANES), F32), pltpu.VMEM((8, LANES), F32),
                        pltpu.VMEM((LANES, MLA_KV_LORA), F32)],
    )
    return pl.pallas_call(
        functools.partial(_mla_decode_kernel, n_pages=n_pages, ntok=ntok),
        grid_spec=grid_spec,
        out_shape=[jax.ShapeDtypeStruct((nseq, ncol, MLA_KV_LORA), F32),
                   jax.ShapeDtypeStruct((nseq, 8, LANES), F32)],
        compiler_params=_cparams(("parallel", "arbitrary")),
    )(page_table, qn, qr, cnew, krnew, *([cache_c] * n_pages), *([cache_krt] * n_pages), wkn, bd)


def _mla_finish_kernel(ctx_ref, l_ref, wv_ref, o_ref):
    o_ref[...] = _dot((ctx_ref[...] / l_ref[...]).astype(BF16), wv_ref[...])


def mla_decode_finish(ctx, lcol, wv):
    m = ctx.shape[0]
    tm = _pick_tile(m, (1024, 512, 256, 128, 64, 32))
    return pl.pallas_call(
        _mla_finish_kernel,
        grid=(m // tm,),
        in_specs=[pl.BlockSpec((tm, ctx.shape[1]), lambda i: (i, 0)),
                  pl.BlockSpec((tm, 1), lambda i: (i, 0)),
                  pl.BlockSpec(wv.shape, lambda i: (0, 0))],
        out_specs=pl.BlockSpec((tm, wv.shape[1]), lambda i: (i, 0)),
        out_shape=jax.ShapeDtypeStruct((m, wv.shape[1]), F32),
        compiler_params=_cparams(("parallel",)),
    )(ctx, lcol, wv)


def _hgrn_gates(fl, qraw, lb):
    f_sig = _sigmoid(fl)
    log_f = jnp.log(lb + (1.0 - lb) * f_sig)
    k = (1.0 - lb) * (1.0 - f_sig)
    q = qraw * _sigmoid(qraw)
    return log_f, k, q


def _hgrn_out(o, graw, gn):
    ms = jnp.mean(o * o, axis=-1, keepdims=True)
    return o * lax.rsqrt(ms + EPS) * gn * (graw * _sigmoid(graw))


def _hgrn_prompt_kernel(f_ref, q_ref, i_ref, g_ref, lb_ref, gn_ref, ltri_ref, o_ref, s_ref, st_ref,
                        *, chunks):
    c = pl.program_id(2)
    cs, sub = HG_CHUNK, HG_SUB
    nsub = cs // sub
    lb = lb_ref[...]
    gn = gn_ref[...]
    ltri = ltri_ref[...]
    row_s = lax.broadcasted_iota(jnp.int32, (sub, 1), 0)
    lane_s = lax.broadcasted_iota(jnp.int32, (sub, cs), 1)
    row_c = lax.broadcasted_iota(jnp.int32, (cs, 1), 0)

    @pl.when(c == 0)
    def _():
        st_ref[...] = jnp.zeros_like(st_ref)

    def chunk(cc, _):
        r0 = pl.multiple_of(cc * cs, cs)
        log_f, k, q = _hgrn_gates(f_ref[pl.ds(r0, cs), :], q_ref[pl.ds(r0, cs), :], lb)
        v = i_ref[pl.ds(r0, cs), :]
        vb = v.astype(BF16)
        hi, mid, lo = _split3(log_f)
        r = _dot(ltri, jnp.concatenate([hi, mid, lo], axis=1))
        b = r[:, :LANES] + r[:, LANES:2 * LANES] + r[:, 2 * LANES:]

        a_rows = []
        for blk in range(nsub):
            lo_r = blk * sub
            b_blk = b[lo_r:lo_r + sub]
            q_blk = q[lo_r:lo_r + sub]
            a = jnp.zeros((sub, cs), F32)
            for s in range(sub):
                e = jnp.exp(jnp.minimum(b_blk - b[lo_r + s:lo_r + s + 1], 0.0))
                col = jnp.sum(q_blk * e * k[lo_r + s:lo_r + s + 1], axis=-1, keepdims=True)
                a = jnp.where(lane_s == lo_r + s, jnp.where(row_s >= s, col, 0.0), a)
            if blk > 0:
                ref = b[lo_r - 1:lo_r]
                qt = (q_blk * jnp.exp(jnp.minimum(b_blk - ref, 0.0))).astype(BF16)
                ks = jnp.where(row_c < lo_r, k * jnp.exp(jnp.minimum(ref - b, 0.0)), 0.0).astype(BF16)
                a = a + _nt_dot(qt, ks)
            a_rows.append(a)
        a_full = jnp.concatenate(a_rows, axis=0)

        st = st_ref[...]
        o = _dot(a_full.astype(BF16), vb) + _nt_dot((q * jnp.exp(b)).astype(BF16), st.astype(BF16))
        b_last = b[cs - 1:cs]
        kd = (k * jnp.exp(b_last - b)).astype(BF16)
        st_ref[...] = st * jnp.exp(b_last) + _tn_dot(vb, kd)
        o_ref[pl.ds(r0, cs), :] = _hgrn_out(o, g_ref[pl.ds(r0, cs), :], gn).astype(o_ref.dtype)
        return 0

    lax.fori_loop(0, chunks, chunk, 0)

    @pl.when(c == pl.num_programs(2) - 1)
    def _():
        s_ref[...] = jnp.transpose(st_ref[...])


def hgrn_prompt(proj, lb, gn, ltri, batch, seq):
    chunks = 2 if seq % (2 * HG_CHUNK) == 0 else 1
    tr = chunks * HG_CHUNK
    nsteps = seq // tr
    blk = lambda col: pl.BlockSpec((tr, LANES), lambda b, h, c: (b * nsteps + c, col + h))
    head = pl.BlockSpec((None, 1, LANES), lambda b, h, c: (h, 0, 0))
    return pl.pallas_call(
        functools.partial(_hgrn_prompt_kernel, chunks=chunks),
        grid=(batch, HG_HEADS, nsteps),
        in_specs=[blk(COL_BF), blk(COL_BQ), blk(COL_BI), blk(COL_BG), head, head,
                  pl.BlockSpec(ltri.shape, lambda b, h, c: (0, 0))],
        out_specs=[pl.BlockSpec((tr, LANES), lambda b, h, c: (b * nsteps + c, h)),
                   pl.BlockSpec((None, None, HG_DK, HG_DV), lambda b, h, c: (b, h, 0, 0))],
        out_shape=[jax.ShapeDtypeStruct((batch * seq, HG_HEADS * HG_DV), BF16),
                   jax.ShapeDtypeStruct((batch, HG_HEADS, HG_DK, HG_DV), F32)],
        scratch_shapes=[pltpu.VMEM((HG_DV, HG_DK), F32)],
        compiler_params=_cparams(("parallel", "parallel", "arbitrary")),
    )(proj, proj, proj, proj, lb, gn, ltri)


def _hgrn_decode_kernel(f_ref, q_ref, i_ref, g_ref, lb_ref, gn_ref, seg_ref, s0_ref, o_ref, s_ref,
                        *, nseq, ntok):
    tr = nseq * ntok
    row = lax.broadcasted_iota(jnp.int32, (tr, 1), 0)
    log_f, k, q = _hgrn_gates(f_ref[...], q_ref[...], lb_ref[...])
    v = i_ref[...]
    hi, mid, lo = _split3(log_f)
    r = _dot(seg_ref[...], jnp.concatenate([hi, mid, lo], axis=1))
    b = r[:, :LANES] + r[:, LANES:2 * LANES] + r[:, 2 * LANES:]
    qe = (q * jnp.exp(b)).astype(BF16)
    vb = v.astype(BF16)
    o = jnp.zeros((tr, HG_DV), F32)
    for n in range(nseq):
        first, last = n * ntok, (n + 1) * ntok - 1
        mine = (row >= first) & (row <= last)
        st = s0_ref[n]
        o = o + jnp.where(mine, _dot(qe, st.astype(BF16)), 0.0)
        for s in range(first, last + 1):
            e = jnp.exp(jnp.minimum(b - b[s:s + 1], 0.0))
            col = jnp.sum(q * e * k[s:s + 1], axis=-1, keepdims=True)
            o = o + jnp.where((row >= s) & (row <= last), col, 0.0) * v[s:s + 1]
        b_last = b[last:last + 1]
        kd = jnp.where(mine, k * jnp.exp(jnp.minimum(b_last - b, 0.0)), 0.0).astype(BF16)
        decay = jnp.transpose(jnp.broadcast_to(jnp.exp(b_last), (LANES, LANES)))
        s_ref[n] = st * decay + _tn_dot(kd, vb)
    o_ref[...] = _hgrn_out(o, g_ref[...], gn_ref[...]).astype(o_ref.dtype)


def hgrn_decode(proj, row0, lb, gn, seg, state, layer, nseq_total, ntok):
    tr = seg.shape[0]
    nseq = tr // ntok
    base = row0 // tr
    blk = lambda col: pl.BlockSpec((tr, LANES), lambda b, h: (base + b, col + h))
    head = pl.BlockSpec((None, 1, LANES), lambda b, h: (h, 0, 0))
    return pl.pallas_call(
        functools.partial(_hgrn_decode_kernel, nseq=nseq, ntok=ntok),
        grid=(nseq_total // nseq, HG_HEADS),
        in_specs=[blk(COL_BF), blk(COL_BQ), blk(COL_BI), blk(COL_BG), head, head,
                  pl.BlockSpec(seg.shape, lambda b, h: (0, 0)),
                  pl.BlockSpec((None, nseq, None, HG_DK, HG_DV), lambda b, h: (layer, b, h, 0, 0))],
        out_specs=[pl.BlockSpec((tr, LANES), lambda b, h: (b, h)),
                   pl.BlockSpec((nseq, None, HG_DK, HG_DV), lambda b, h: (b, h, 0, 0))],
        out_shape=[jax.ShapeDtypeStruct((nseq_total * ntok, HG_HEADS * HG_DV), BF16),
                   jax.ShapeDtypeStruct((nseq_total, HG_HEADS, HG_DK, HG_DV), F32)],
        compiler_params=_cparams(("parallel", "parallel")),
    )(proj, proj, proj, proj, lb, gn, seg, state)


def _pack_w_in(w):
    d = w.shape[0]
    sizes = (512, 128, 128, 512, 512, 512, 512, MLA_Q_LORA, MLA_KV_LORA, MLA_D_ROPE, N_BRANCH * d)
    offs = [0]
    for s in sizes:
        offs.append(offs[-1] + s)
    a_q, a_k, a_v, b_f, b_q, b_i, b_g, c_q, c_kv, c_kr, gates = (w[:, offs[n]:offs[n + 1]] for n in range(len(sizes)))
    kr_blk = jnp.concatenate([jnp.zeros((d, MLA_D_NOPE), w.dtype), c_kr,
                              jnp.zeros((d, LANES - MLA_D_NOPE - MLA_D_ROPE), w.dtype)], axis=1)
    return jnp.concatenate([a_q, b_f, b_q, b_i, b_g, a_k, a_v, c_kv, gates, c_q, kr_blk], axis=1).astype(BF16)


def _pad_heads(w, width):
    r = w.shape[0]
    w = w.reshape(r, MLA_HEADS, width)
    return jnp.pad(w, ((0, 0), (0, 0), (0, LANES - width))).reshape(r, MLA_HEADS * LANES)


def _rope_tables(pos):
    half = MLA_D_ROPE // 2
    inv = ROPE_BASE ** (-jnp.arange(half, dtype=F32) / half)
    ang = pos.astype(F32)[:, None] * inv[None, :]
    c, s = jnp.cos(ang), jnp.sin(ang)
    n = pos.shape[0]
    tail = LANES - MLA_D_NOPE - MLA_D_ROPE
    cos = jnp.concatenate([jnp.ones((n, MLA_D_NOPE), F32), c, c, jnp.ones((n, tail), F32)], axis=1)
    sin = jnp.concatenate([jnp.zeros((n, MLA_D_NOPE), F32), -s, s, jnp.zeros((n, tail), F32)], axis=1)
    return cos, sin


def _lane_gain(nope, rope):
    tail = LANES - MLA_D_NOPE - MLA_D_ROPE
    return jnp.concatenate([nope, rope, jnp.zeros((tail,), F32)])[None, :]


def kernel(x_prompt, x_sample, cache_sb_k, cache_sb_v, cache_mla_c, cache_mla_kr, state_hgrn, page_table,
           norm1_g, w_in, hg_lower_bound, hg_out_norm_g, mla_q_norm_g, mla_kv_norm_g, mla_w_uq, mla_w_ukv,
           mla_qn_g, mla_kn_g, mla_qr_g, mla_kr_g, w_br_a, w_br_b, w_br_c, w_out, norm2_g, w_up, w_down):
    batch, seq, d = x_prompt.shape
    nseq, ntok, _ = x_sample.shape
    depth = w_in.shape[0]
    npg = page_table.shape[1]
    page = cache_sb_k.shape[2]
    past_len = npg * page
    mp = batch * seq
    ms = nseq * ntok
    dh = SB_HEAD_DIM
    ncol = SB_HEADS * ntok

    lb_soft = jax.nn.softmax(hg_lower_bound.astype(F32), axis=0)
    lower_bounds = jnp.cumsum(lb_soft, axis=0) - lb_soft[0]

    pos = jnp.concatenate([jnp.tile(jnp.arange(seq, dtype=jnp.int32), batch),
                           jnp.tile(past_len + jnp.arange(ntok, dtype=jnp.int32), nseq)])
    cos, sin = _rope_tables(pos)

    ii = jnp.arange(LANES)
    u_strict = (ii[:, None] > ii[None, :])
    u2 = jnp.concatenate([u_strict, jnp.ones((LANES, LANES), bool)], axis=1).astype(BF16)
    ltri = (ii[:, None] >= ii[None, :]).astype(BF16)[:HG_CHUNK, :HG_CHUNK]
    assert ncol <= LANES and mp % 16 == 0 and 16 % ntok == 0 and page == LANES
    bd = (jnp.arange(MLA_HEADS * MLA_D_NOPE)[:, None] // MLA_D_NOPE
          == ii[None, :] // ntok).astype(BF16)
    i16 = jnp.arange(16)
    seg = ((i16[:, None] // ntok == i16[None, :] // ntok) & (i16[:, None] >= i16[None, :])).astype(BF16)
    cpad = ((0, 0), (0, 0), (0, LANES - ncol))

    ckt = cache_sb_k.transpose(0, 1, 3, 4, 2).reshape(cache_sb_k.shape[:2] + (SB_KV_HEADS * dh, page))
    cvt = cache_sb_v.transpose(0, 1, 3, 4, 2).reshape(cache_sb_v.shape[:2] + (SB_KV_HEADS * dh, page))
    ckrt = cache_mla_kr.transpose(0, 1, 3, 2)

    h = jnp.concatenate([x_prompt.reshape(mp, d), x_sample.reshape(ms, d)], axis=0)
    mla_scale = 1.0 / math.sqrt(MLA_D_NOPE + MLA_D_ROPE)
    outs = [[] for _ in range(10)]

    for l in range(depth):
        w_in_p = _pack_w_in(w_in[l])
        wuq = _pad_heads(mla_w_uq[l], MLA_D_NOPE + MLA_D_ROPE).astype(BF16)
        wukv = mla_w_ukv[l].reshape(MLA_KV_LORA, MLA_HEADS, MLA_D_NOPE + MLA_D_V)
        wkn = wukv[:, :, :MLA_D_NOPE].reshape(MLA_KV_LORA, MLA_HEADS * MLA_D_NOPE)
        wkn_pad = _pad_heads(wkn, MLA_D_NOPE).astype(BF16)
        wkn = wkn.astype(BF16)
        wv = wukv[:, :, MLA_D_NOPE:].reshape(MLA_KV_LORA, MLA_HEADS * MLA_D_V).astype(BF16)
        qgain = _lane_gain(mla_qn_g[l] * mla_kn_g[l] * mla_scale, mla_qr_g[l] * mla_scale)
        krgain = _lane_gain(jnp.zeros((MLA_D_NOPE,), F32), mla_kr_g[l])
        lb = lower_bounds[l].reshape(HG_HEADS, 1, HG_DK)
        gn = hg_out_norm_g[l].reshape(HG_HEADS, 1, HG_DV)

        proj = rms_matmul(h, norm1_g[l][None, :], w_in_p)
        q128, ckv, kr, k128, vmla = mla_prep(proj, mla_q_norm_g[l][None, :], mla_kv_norm_g[l][None, :],
                                             qgain, krgain, cos, sin, wuq, wkn_pad, wv)

        oa_p = sb_attn_prompt(proj, u2, batch, seq)
        oc_p = mla_attn_prompt(q128, k128, vmla, batch, seq)
        ob_p, st_p = hgrn_prompt(proj, lb, gn, ltri, batch, seq)

        kv_s = proj[mp:, COL_AKV * LANES:(COL_AKV + 2) * LANES].reshape(nseq, ntok, 2 * LANES)
        k_new, v_new = kv_s[..., :LANES], kv_s[..., LANES:]
        rpad = ((0, 0), (0, 8 - ntok), (0, 0))
        qs = proj[mp:, :SB_HEADS * dh].reshape(nseq, ntok, SB_KV_HEADS, SB_GROUP, dh) * (1.0 / math.sqrt(dh))
        qs = qs.transpose(0, 2, 3, 1, 4).reshape(nseq, SB_KV_HEADS, SB_GROUP * ntok, dh)
        q_rows = jnp.einsum("bkcd,kj->bkcjd", qs, jnp.eye(SB_KV_HEADS, dtype=F32))
        q_rows = q_rows.reshape(nseq, ncol, SB_KV_HEADS * dh).astype(BF16)
        oa_raw = sb_attn_decode(page_table, q_rows, jnp.pad(k_new, rpad), jnp.pad(v_new, rpad),
                                ckt, cvt, l, u2, ntok)
        oa_s = oa_raw.reshape(nseq, SB_KV_HEADS, SB_GROUP, ntok, SB_KV_HEADS, dh)
        oa_s = jnp.stack([oa_s[:, kk, :, :, kk] for kk in range(SB_KV_HEADS)], axis=1)
        oa_s = oa_s.transpose(0, 3, 1, 2, 4).reshape(ms, SB_HEADS * dh).astype(BF16)

        q_s = q128[mp:].reshape(nseq, ntok, MLA_HEADS, LANES)
        qn_s = q_s[..., :MLA_D_NOPE].transpose(0, 2, 3, 1)
        qn_bd = jnp.einsum("bhdt,hj->bhdjt", qn_s, jnp.eye(MLA_HEADS, dtype=qn_s.dtype))
        qn_bd = qn_bd.reshape(nseq, MLA_HEADS * MLA_D_NOPE, ncol)
        qr_s = q_s[..., MLA_D_NOPE:MLA_D_NOPE + MLA_D_ROPE].transpose(0, 3, 2, 1).reshape(nseq, MLA_D_ROPE, ncol)
        rpad16 = ((0, 0), (0, 16 - ntok), (0, 0))
        c_new = jnp.pad(ckv[mp:].reshape(nseq, ntok, MLA_KV_LORA), rpad16)
        kr_new = jnp.pad(kr[mp:].reshape(nseq, ntok, MLA_D_ROPE), rpad16)
        ctx, lsum = mla_attn_decode(page_table, jnp.pad(qn_bd, cpad), jnp.pad(qr_s, cpad), c_new, kr_new,
                                    cache_mla_c, ckrt, l, wkn, bd, ntok)
        oc_all = mla_decode_finish(ctx.reshape(nseq * ncol, MLA_KV_LORA),
                                   lsum[:, 0, :ncol].reshape(nseq * ncol, 1), wv)
        oc_all = oc_all.reshape(nseq, MLA_HEADS, ntok, MLA_HEADS, MLA_D_V)
        oc_s = jnp.stack([oc_all[:, hh, :, hh] for hh in range(MLA_HEADS)], axis=2)
        oc_s = oc_s.reshape(ms, MLA_HEADS * MLA_D_V).astype(BF16)

        ob_s, st_s = hgrn_decode(proj, mp, lb, gn, seg, state_hgrn, l, nseq, ntok)

        oa = jnp.concatenate([oa_p, oa_s], axis=0)
        ob = jnp.concatenate([ob_p, ob_s], axis=0)
        oc = jnp.concatenate([oc_p, oc_s], axis=0)
        h = merge_out(h, oa, ob, oc, proj, w_br_a[l].astype(BF16), w_br_b[l].astype(BF16),
                      w_br_c[l].astype(BF16), w_out[l].astype(BF16))
        h = mlp(h, norm2_g[l][None, :], w_up[l].astype(BF16), w_down[l].astype(BF16))

        kv_p = proj[:mp, COL_AKV * LANES:(COL_AKV + 2) * LANES]
        outs[0].append(kv_p[:, :LANES].reshape(batch, seq, SB_KV_HEADS, dh))
        outs[1].append(kv_p[:, LANES:].reshape(batch, seq, SB_KV_HEADS, dh))
        outs[2].append(ckv[:mp].reshape(batch, seq, MLA_KV_LORA))
        outs[3].append(kr[:mp].reshape(batch, seq, MLA_D_ROPE))
        outs[4].append(st_p)
        outs[5].append(k_new.reshape(nseq, ntok, SB_KV_HEADS, dh))
        outs[6].append(v_new.reshape(nseq, ntok, SB_KV_HEADS, dh))
        outs[7].append(ckv[mp:].reshape(nseq, ntok, MLA_KV_LORA))
        outs[8].append(kr[mp:].reshape(nseq, ntok, MLA_D_ROPE))
        outs[9].append(st_s)

    return (h[:mp].reshape(batch, seq, d), h[mp:].reshape(nseq, ntok, d)) + tuple(jnp.stack(o) for o in outs)
```
